```python
import math
import jax
import jax.numpy as jnp
from jax import lax
import numpy as np

D_MODEL = 1024
BATCH = 16
SEQ = 4096
DEPTH = 2

SSM_GROUP = 16
SSM_GROUPS = 32
SSM_WIDTH = SSM_GROUP * SSM_GROUPS
SSM_STATE = 64
N_HEADS = 8
HEAD_DIM = 64
V_DIM = 2 * HEAD_DIM
QK_WIDTH = N_HEADS * 2 * HEAD_DIM
ATTN_WIDTH = N_HEADS * V_DIM
Q_BLOCK = 128
ROPE_THETA = 10000.0
IN_COLS = SSM_WIDTH + 2 * QK_WIDTH + ATTN_WIDTH + 2 * D_MODEL
SPLITS = [SSM_WIDTH,
          SSM_WIDTH + QK_WIDTH,
          SSM_WIDTH + 2 * QK_WIDTH,
          SSM_WIDTH + 2 * QK_WIDTH + ATTN_WIDTH,
          SSM_WIDTH + 2 * QK_WIDTH + ATTN_WIDTH + D_MODEL]
D_FF = 2816
N_EXPERTS = 8
TOP_K = 2
N_DENSE = (DEPTH + 1) // 2
N_MOE = DEPTH // 2
EPS = 1e-6
SUBLN_EPS = 1e-5

kernel_name = 'hybrid_s5_diffattn_moe_trunk'


def rms_norm(x, g, eps=EPS):
    x32 = x.astype(jnp.float32)
    y = x32 * lax.rsqrt(jnp.mean(x32 * x32, axis=-1, keepdims=True) + eps)
    return (y * g.astype(jnp.float32)).astype(x.dtype)


def apply_rope(t, cos, sin):
    t32 = t.astype(jnp.float32)
    t1, t2 = jnp.split(t32, 2, axis=-1)
    rot = jnp.concatenate([-t2, t1], axis=-1)
    return (t32 * cos + rot * sin).astype(t.dtype)


def _complex_affine_combine(c1, c2):
    a1r, a1i, b1r, b1i = c1
    a2r, a2i, b2r, b2i = c2
    ar = a2r * a1r - a2i * a1i
    ai = a2r * a1i + a2i * a1r
    br = a2r * b1r - a2i * b1i + b2r
    bi = a2r * b1i + a2i * b1r + b2i
    return (ar, ai, br, bi)


def s5_branch(u, lam_re, lam_im, log_dt, b_re, b_im, c_re, c_im, d):
    f32 = jnp.float32
    bsz, s, _ = u.shape
    u32 = u.astype(f32).reshape(bsz, s, SSM_GROUPS, SSM_GROUP)
    lr = lam_re.astype(f32)
    li = lam_im.astype(f32)
    dt = jnp.exp(log_dt.astype(f32))[:, None]
    mag = jnp.exp(lr * dt)
    ar = mag * jnp.cos(li * dt)
    ai = mag * jnp.sin(li * dt)
    den = lr * lr + li * li
    nr = ar - 1.0
    ni = ai
    fr = (nr * lr + ni * li) / den
    fi = (ni * lr - nr * li) / den
    br = b_re.astype(f32)
    bi = b_im.astype(f32)
    bbr = fr[..., None] * br - fi[..., None] * bi
    bbi = fr[..., None] * bi + fi[..., None] * br
    bu_r = jnp.einsum('gpj,bsgj->bsgp', bbr, u32)
    bu_i = jnp.einsum('gpj,bsgj->bsgp', bbi, u32)
    a_r = jnp.broadcast_to(ar, (1, s) + ar.shape)
    a_i = jnp.broadcast_to(ai, (1, s) + ai.shape)
    _, _, st_r, st_i = lax.associative_scan(_complex_affine_combine, (a_r, a_i, bu_r, bu_i), axis=1)
    y = (jnp.einsum('gjp,bsgp->bsgj', c_re.astype(f32), st_r)
         - jnp.einsum('gjp,bsgp->bsgj', c_im.astype(f32), st_i)
         + d.astype(f32) * u32)
    return y.reshape(bsz, s, SSM_WIDTH).astype(u.dtype)


def diff_attention(q, k, v, lam):
    bsz, s = q.shape[0], q.shape[1]
    n_blk = s // Q_BLOCK
    scale = HEAD_DIM ** -0.5
    kpos = jnp.arange(s)
    neg = jnp.finfo(jnp.float32).min

    def one_block(i):
        qs = lax.dynamic_slice_in_dim(q, i * Q_BLOCK, Q_BLOCK, axis=1)
        sc = jnp.einsum('bqhcd,bkhcd->bhcqk', qs, k).astype(jnp.float32) * scale
        qpos = i * Q_BLOCK + jnp.arange(Q_BLOCK)
        mask = qpos[:, None] >= kpos[None, :]
        p = jax.nn.softmax(jnp.where(mask, sc, neg), axis=-1)
        w = p[:, :, 0] - lam * p[:, :, 1]
        return jnp.einsum('bhqk,bkhd->bqhd', w.astype(v.dtype), v)

    out = lax.map(one_block, jnp.arange(n_blk))
    return jnp.moveaxis(out, 0, 1).reshape(bsz, s, N_HEADS, V_DIM)


def swiglu(h, wg, wu, wd):
    return (jax.nn.silu(h @ wg) * (h @ wu)) @ wd


def moe_swiglu(h, router_w, wg, wu, wd):
    bsz, s, dm = h.shape
    hf = h.reshape(-1, dm)
    logits = hf.astype(jnp.float32) @ router_w.astype(jnp.float32)
    top_vals, top_idx = lax.top_k(logits, TOP_K)
    top_w = jax.nn.softmax(top_vals, axis=-1)
    gates = jnp.sum(jax.nn.one_hot(top_idx, N_EXPERTS, dtype=jnp.float32) * top_w[..., None], axis=1)
    gates = gates.astype(h.dtype)
    y = jnp.zeros_like(hf)
    for e in range(N_EXPERTS):
        y = y + gates[:, e:e + 1] * swiglu(hf, wg[e], wu[e], wd[e])
    return y.reshape(bsz, s, dm)


def setup_inputs(seed: int = 0) -> dict:
    key = jax.random.key(seed)
    ks = jax.random.split(key, 32)
    f32 = jnp.float32
    L, D, G, P, J = DEPTH, D_MODEL, SSM_GROUPS, SSM_STATE, SSM_GROUP

    def nrm(k, shape, std):
        return jax.random.normal(k, shape, f32) * std

    x = jax.random.normal(ks[0], (BATCH, SEQ, D), f32)
    positions = jnp.broadcast_to(jnp.arange(SEQ, dtype=jnp.int32), (BATCH, SEQ))
    mix_norm = 1.0 + nrm(ks[1], (L, D), 0.02)
    w_in = nrm(ks[2], (L, D, IN_COLS), D ** -0.5)
    ssm_lambda_re = -0.5 + nrm(ks[3], (L, G, P), 0.01)
    ssm_lambda_im = math.pi * jnp.arange(P, dtype=f32)[None, None, :] + nrm(ks[4], (L, G, P), 0.01)
    ssm_log_dt = jax.random.uniform(ks[5], (L, G), f32, math.log(1e-3), math.log(1e-1))
    ssm_b_re = nrm(ks[6], (L, G, P, J), (2.0 * J) ** -0.5)
    ssm_b_im = nrm(ks[7], (L, G, P, J), (2.0 * J) ** -0.5)
    ssm_c_re = nrm(ks[8], (L, G, J, P), (2.0 * P) ** -0.5 * 4.0)
    ssm_c_im = nrm(ks[9], (L, G, J, P), (2.0 * P) ** -0.5 * 4.0)
    ssm_d = nrm(ks[10], (L, G, J), 1.0)
    w_glu = nrm(ks[11], (L, SSM_WIDTH, SSM_WIDTH), SSM_WIDTH ** -0.5)
    b_glu = nrm(ks[12], (L, SSM_WIDTH), 0.01)
    lambda_q1 = nrm(ks[13], (L, HEAD_DIM), 0.1)
    lambda_k1 = nrm(ks[14], (L, HEAD_DIM), 0.1)
    lambda_q2 = nrm(ks[15], (L, HEAD_DIM), 0.1)
    lambda_k2 = nrm(ks[16], (L, HEAD_DIM), 0.1)
    subln = 1.0 + nrm(ks[17], (L, V_DIM), 0.02)
    w_proj_ssm = nrm(ks[18], (L, SSM_WIDTH, D), SSM_WIDTH ** -0.5)
    w_proj_attn = nrm(ks[19], (L, ATTN_WIDTH, D), ATTN_WIDTH ** -0.5)
    w_out = nrm(ks[20], (L, D, D), D ** -0.5)
    ffn_norm = 1.0 + nrm(ks[21], (L, D), 0.02)
    dense_w_gate = nrm(ks[22], (N_DENSE, D, D_FF), D ** -0.5)
    dense_w_up = nrm(ks[23], (N_DENSE, D, D_FF), D ** -0.5)
    dense_w_down = nrm(ks[24], (N_DENSE, D_FF, D), D_FF ** -0.5)
    router_w = nrm(ks[25], (N_MOE, D, N_EXPERTS), D ** -0.5)
    moe_w_gate = nrm(ks[26], (N_MOE, N_EXPERTS, D, D_FF), D ** -0.5)
    moe_w_up = nrm(ks[27], (N_MOE, N_EXPERTS, D, D_FF), D ** -0.5)
    moe_w_down = nrm(ks[28], (N_MOE, N_EXPERTS, D_FF, D), D_FF ** -0.5)
    final_norm = 1.0 + nrm(ks[29], (D,), 0.02)
    return {'x': x, 'positions': positions, 'mix_norm': mix_norm, 'w_in': w_in,
            'ssm_lambda_re': ssm_lambda_re, 'ssm_lambda_im': ssm_lambda_im, 'ssm_log_dt': ssm_log_dt,
            'ssm_b_re': ssm_b_re, 'ssm_b_im': ssm_b_im, 'ssm_c_re': ssm_c_re, 'ssm_c_im': ssm_c_im,
            'ssm_d': ssm_d, 'w_glu': w_glu, 'b_glu': b_glu,
            'lambda_q1': lambda_q1, 'lambda_k1': lambda_k1, 'lambda_q2': lambda_q2, 'lambda_k2': lambda_k2,
            'subln': subln, 'w_proj_ssm': w_proj_ssm, 'w_proj_attn': w_proj_attn, 'w_out': w_out,
            'ffn_norm': ffn_norm, 'dense_w_gate': dense_w_gate, 'dense_w_up': dense_w_up,
            'dense_w_down': dense_w_down, 'router_w': router_w, 'moe_w_gate': moe_w_gate,
            'moe_w_up': moe_w_up, 'moe_w_down': moe_w_down, 'final_norm': final_norm}


def reference(x, positions, mix_norm, w_in, ssm_lambda_re, ssm_lambda_im, ssm_log_dt,
              ssm_b_re, ssm_b_im, ssm_c_re, ssm_c_im, ssm_d, w_glu, b_glu,
              lambda_q1, lambda_k1, lambda_q2, lambda_k2, subln, w_proj_ssm, w_proj_attn, w_out,
              ffn_norm, dense_w_gate, dense_w_up, dense_w_down, router_w, moe_w_gate,
              moe_w_up, moe_w_down, final_norm):
    bsz, s, _ = x.shape
    f32 = jnp.float32
    inv_freq = ROPE_THETA ** (-jnp.arange(0, HEAD_DIM, 2, dtype=f32) / HEAD_DIM)
    ang = positions.astype(f32)[..., None] * inv_freq
    emb = jnp.concatenate([ang, ang], axis=-1)[:, :, None, None, :]
    cos = jnp.cos(emb)
    sin = jnp.sin(emb)

    for l in range(DEPTH):
        h = rms_norm(x, mix_norm[l])
        z = h @ w_in[l]
        u, q, k, v, g_ssm, g_attn = jnp.split(z, SPLITS, axis=-1)

        y_a = s5_branch(u, ssm_lambda_re[l], ssm_lambda_im[l], ssm_log_dt[l], ssm_b_re[l], ssm_b_im[l],
                        ssm_c_re[l], ssm_c_im[l], ssm_d[l])
        y_a = jax.nn.gelu(y_a)
        y_a = y_a * jax.nn.sigmoid(y_a @ w_glu[l] + b_glu[l])

        q = apply_rope(q.reshape(bsz, s, N_HEADS, 2, HEAD_DIM), cos, sin)
        k = apply_rope(k.reshape(bsz, s, N_HEADS, 2, HEAD_DIM), cos, sin)
        v = v.reshape(bsz, s, N_HEADS, V_DIM)
        lam_init = 0.8 - 0.6 * math.exp(-0.3 * l)
        lam = (jnp.exp(jnp.sum(lambda_q1[l].astype(f32) * lambda_k1[l].astype(f32)))
               - jnp.exp(jnp.sum(lambda_q2[l].astype(f32) * lambda_k2[l].astype(f32))) + lam_init)
        o = diff_attention(q, k, v, lam)
        o = rms_norm(o, subln[l], SUBLN_EPS) * (1.0 - lam_init)
        y_b = o.reshape(bsz, s, ATTN_WIDTH)

        m = jax.nn.sigmoid(g_ssm) * (y_a @ w_proj_ssm[l]) + jax.nn.sigmoid(g_attn) * (y_b @ w_proj_attn[l])
        x = x + m @ w_out[l]

        h = rms_norm(x, ffn_norm[l])
        if l % 2 == 0:
            i = l // 2
            f = swiglu(h, dense_w_gate[i], dense_w_up[i], dense_w_down[i])
        else:
            i = l // 2
            f = moe_swiglu(h, router_w[i], moe_w_gate[i], moe_w_up[i], moe_w_down[i])
        x = x + f

    return rms_norm(x, final_norm)
```

```python
import functools
import math

import jax
import jax.numpy as jnp
from jax import lax
from jax.experimental import pallas as pl
from jax.experimental.pallas import tpu as pltpu

F32 = jnp.float32
BF16 = jnp.bfloat16

D_MODEL = 1024
SSM_GROUP = 16
SSM_GROUPS = 32
SSM_WIDTH = SSM_GROUP * SSM_GROUPS
SSM_STATE = 64
N_HEADS = 8
HEAD_DIM = 64
V_DIM = 2 * HEAD_DIM
QK_WIDTH = N_HEADS * 2 * HEAD_DIM
ATTN_WIDTH = N_HEADS * V_DIM
ROPE_THETA = 10000.0
IN_COLS = SSM_WIDTH + 2 * QK_WIDTH + ATTN_WIDTH + 2 * D_MODEL
D_FF = 2816
N_EXPERTS = 8
EPS = 1e-6
SUBLN_EPS = 1e-5

LANES = 128
HALF_GROUPS = SSM_GROUPS // 2
HALF_IN = HALF_GROUPS * SSM_GROUP
HALF_STATE = HALF_GROUPS * SSM_STATE
VMEM_LIMIT = 56 * 1024 * 1024

ROW_TILE = 512
S5_STEPS = 32
ATTN_TQ = 256
ATTN_TK = 256
FF_CHUNK = 256
NEG_BIG = -1e30


def _dot(a, b):
    return jnp.dot(a, b, preferred_element_type=F32)


def _rms(x, g, eps):
    return x * lax.rsqrt(jnp.mean(x * x, axis=-1, keepdims=True) + eps) * g


def _params(sem):
    return pltpu.CompilerParams(dimension_semantics=sem, vmem_limit_bytes=VMEM_LIMIT)


def _inproj_kernel(x_ref, g_ref, w_ref, cos_ref, sin_ref,
                   u_ref, q_ref, k_ref, v_ref, gs_ref, ga_ref):
    h = _rms(x_ref[...], g_ref[...], EPS).astype(BF16)
    cos = cos_ref[...]
    sin = sin_ref[...]
    lane = lax.broadcasted_iota(jnp.int32, (1, LANES), 1)
    first_half = (lane % HEAD_DIM) < (HEAD_DIM // 2)

    def rope(t):
        partner = jnp.where(first_half,
                            pltpu.roll(t, LANES - HEAD_DIM // 2, 1),
                            pltpu.roll(t, HEAD_DIM // 2, 1))
        return t * cos + partner * sin

    u_ref[...] = _dot(h, w_ref[:, 0:SSM_WIDTH])
    chunk = 512
    q_scale = HEAD_DIM ** -0.5
    for c in range(QK_WIDTH // chunk):
        base = SSM_WIDTH + c * chunk
        zq = _dot(h, w_ref[:, base:base + chunk])
        zk = _dot(h, w_ref[:, base + QK_WIDTH:base + QK_WIDTH + chunk])
        for j in range(chunk // LANES):
            sl = slice(j * LANES, (j + 1) * LANES)
            dst = slice(c * chunk + j * LANES, c * chunk + (j + 1) * LANES)
            q_ref[:, dst] = (rope(zq[:, sl]) * q_scale).astype(BF16)
            k_ref[:, dst] = rope(zk[:, sl]).astype(BF16)
    vbase = SSM_WIDTH + 2 * QK_WIDTH
    for c in range(ATTN_WIDTH // chunk):
        sl = slice(c * chunk, (c + 1) * chunk)
        v_ref[:, sl] = _dot(h, w_ref[:, vbase + c * chunk:vbase + (c + 1) * chunk]).astype(BF16)
    gbase = vbase + ATTN_WIDTH
    for c in range(D_MODEL // chunk):
        sl = slice(c * chunk, (c + 1) * chunk)
        zs = _dot(h, w_ref[:, gbase + c * chunk:gbase + (c + 1) * chunk])
        za = _dot(h, w_ref[:, gbase + D_MODEL + c * chunk:gbase + D_MODEL + (c + 1) * chunk])
        gs_ref[:, sl] = jax.nn.sigmoid(zs).astype(BF16)
        ga_ref[:, sl] = jax.nn.sigmoid(za).astype(BF16)


def _in_projection(x, g, w_bf16, cos_t, sin_t):
    n = x.shape[0]
    tm = ROW_TILE
    row = lambda w: pl.BlockSpec((tm, w), lambda i: (i, 0))
    full = lambda a: pl.BlockSpec(a.shape, lambda i: (0,) * a.ndim)
    return pl.pallas_call(
        _inproj_kernel,
        grid=(n // tm,),
        in_specs=[row(D_MODEL), full(g), full(w_bf16), row(LANES), row(LANES)],
        out_specs=[row(SSM_WIDTH), row(QK_WIDTH), row(QK_WIDTH), row(ATTN_WIDTH),
                   row(D_MODEL), row(D_MODEL)],
        out_shape=[jax.ShapeDtypeStruct((n, SSM_WIDTH), F32),
                   jax.ShapeDtypeStruct((n, QK_WIDTH), BF16),
                   jax.ShapeDtypeStruct((n, QK_WIDTH), BF16),
                   jax.ShapeDtypeStruct((n, ATTN_WIDTH), BF16),
                   jax.ShapeDtypeStruct((n, D_MODEL), BF16),
                   jax.ShapeDtypeStruct((n, D_MODEL), BF16)],
        compiler_params=_params(("parallel",)),
        name="in_projection",
    )(x, g, w_bf16, cos_t, sin_t)


def _gelu_tanh(x):
    return 0.5 * x * (1.0 + jnp.tanh(math.sqrt(2.0 / math.pi) * (x + 0.044715 * (x * x * x))))


def _s5_kernel(u_ref, bblk_ref, cblk_ref, a_ref, d_ref, wglu_ref, bglu_ref,
               ya_ref, state_ref, bu_ref, s_ref, *, steps, batch):
    @pl.when(pl.program_id(0) == 0)
    def _():
        state_ref[...] = jnp.zeros_like(state_ref)

    u = u_ref[...]
    ub = u.astype(BF16)
    half_cols = 2 * HALF_STATE
    for half in range(2):
        bu_ref[:, half * half_cols:(half + 1) * half_cols] = _dot(
            ub[:, half * HALF_IN:(half + 1) * HALF_IN], bblk_ref[half])

    width = 512
    for half in range(2):
        for j in range(HALF_STATE // width):
            cre = half * half_cols + j * width
            cim = cre + HALF_STATE
            a_re = jnp.broadcast_to(a_ref[2 * half:2 * half + 1, j * width:(j + 1) * width],
                                    (batch, width))
            a_im = jnp.broadcast_to(a_ref[2 * half + 1:2 * half + 2, j * width:(j + 1) * width],
                                    (batch, width))

            def step(t, carry, cre=cre, cim=cim, a_re=a_re, a_im=a_im):
                s_re, s_im = carry
                r0 = pl.multiple_of(t * batch, batch)
                n_re = a_re * s_re - a_im * s_im + bu_ref[pl.ds(r0, batch), cre:cre + width]
                n_im = a_re * s_im + a_im * s_re + bu_ref[pl.ds(r0, batch), cim:cim + width]
                s_ref[pl.ds(r0, batch), cre:cre + width] = n_re.astype(BF16)
                s_ref[pl.ds(r0, batch), cim:cim + width] = n_im.astype(BF16)
                return n_re, n_im

            s_re, s_im = lax.fori_loop(
                0, steps, step,
                (state_ref[:, cre:cre + width], state_ref[:, cim:cim + width]), unroll=4)
            state_ref[:, cre:cre + width] = s_re
            state_ref[:, cim:cim + width] = s_im

    y = jnp.concatenate(
        [_dot(s_ref[:, h * half_cols:(h + 1) * half_cols], cblk_ref[h]) for h in range(2)], axis=1)
    y = _gelu_tanh(y + d_ref[...] * u)
    z = _dot(y.astype(BF16), wglu_ref[...]) + bglu_ref[...]
    ya_ref[...] = (y * jax.nn.sigmoid(z)).astype(BF16)


def _s5_discretise(lam_re, lam_im, log_dt, b_re, b_im, c_re, c_im):
    lr = lam_re.astype(F32)
    li = lam_im.astype(F32)
    dt = jnp.exp(log_dt.astype(F32))[:, None]
    mag = jnp.exp(lr * dt)
    ar = mag * jnp.cos(li * dt)
    ai = mag * jnp.sin(li * dt)
    den = lr * lr + li * li
    nr = ar - 1.0
    ni = ai
    fr = (nr * lr + ni * li) / den
    fi = (ni * lr - nr * li) / den
    br = b_re.astype(F32)
    bi = b_im.astype(F32)
    bbr = fr[..., None] * br - fi[..., None] * bi
    bbi = fr[..., None] * bi + fi[..., None] * br
    eye = jnp.eye(HALF_GROUPS, dtype=F32)

    def in_block(w):
        w = w.reshape(2, HALF_GROUPS, SSM_STATE, SSM_GROUP)
        blk = jnp.einsum('hgpj,gk->hgjkp', w, eye)
        return blk.reshape(2, HALF_IN, HALF_STATE)

    def out_block(w):
        w = w.reshape(2, HALF_GROUPS, SSM_GROUP, SSM_STATE)
        blk = jnp.einsum('hgjp,gk->hgpkj', w, eye)
        return blk.reshape(2, HALF_STATE, HALF_IN)

    bblk = jnp.concatenate([in_block(bbr), in_block(bbi)], axis=2).astype(BF16)
    cblk = jnp.concatenate([out_block(c_re.astype(F32)), out_block(-c_im.astype(F32))],
                           axis=1).astype(BF16)
    a_vec = jnp.stack([ar.reshape(2, HALF_STATE)[0], ai.reshape(2, HALF_STATE)[0],
                       ar.reshape(2, HALF_STATE)[1], ai.reshape(2, HALF_STATE)[1]], axis=0)
    return bblk, cblk, a_vec


def _s5_branch(u_tm, bblk, cblk, a_vec, d_row, wglu_bf16, bglu_row, batch):
    n = u_tm.shape[0]
    rows = S5_STEPS * batch
    full = lambda a: pl.BlockSpec(a.shape, lambda i: (0,) * a.ndim)
    kern = functools.partial(_s5_kernel, steps=S5_STEPS, batch=batch)
    return pl.pallas_call(
        kern,
        grid=(n // rows,),
        in_specs=[pl.BlockSpec((rows, SSM_WIDTH), lambda i: (i, 0)),
                  full(bblk), full(cblk), full(a_vec), full(d_row), full(wglu_bf16), full(bglu_row)],
        out_specs=pl.BlockSpec((rows, SSM_WIDTH), lambda i: (i, 0)),
        out_shape=jax.ShapeDtypeStruct((n, SSM_WIDTH), BF16),
        scratch_shapes=[pltpu.VMEM((batch, 4 * HALF_STATE), F32),
                        pltpu.VMEM((rows, 4 * HALF_STATE), F32),
                        pltpu.VMEM((rows, 4 * HALF_STATE), BF16)],
        compiler_params=_params(("arbitrary",)),
        name="s5_branch",
    )(u_tm, bblk, cblk, a_vec, d_row, wglu_bf16, bglu_row)


def _attn_kernel(lam_ref, q_ref, k_ref, v_ref, sub_ref, o_ref,
                 m_ref, l_ref, acc_ref, *, tq, tk, out_scale):
    qi = pl.program_id(2)
    q = q_ref[0]
    lane = lax.broadcasted_iota(jnp.int32, (1, LANES), 1)
    zero = jnp.zeros_like(q)
    q_comp = (jnp.where(lane < HEAD_DIM, q, zero), jnp.where(lane >= HEAD_DIM, q, zero))

    m_ref[...] = jnp.full_like(m_ref, NEG_BIG)
    l_ref[...] = jnp.zeros_like(l_ref)
    acc_ref[...] = jnp.zeros_like(acc_ref)

    def block(ki, masked):
        k0 = pl.multiple_of(ki * tk, tk)
        k = k_ref[0, pl.ds(k0, tk), :]
        v = v_ref[0, pl.ds(k0, tk), :]
        for c in range(2):
            s = lax.dot_general(q_comp[c], k, (((1,), (1,)), ((), ())),
                                preferred_element_type=F32)
            if masked:
                row = lax.broadcasted_iota(jnp.int32, (tq, tk), 0)
                col = lax.broadcasted_iota(jnp.int32, (tq, tk), 1)
                s = jnp.where(row >= col, s, NEG_BIG)
            m_old = m_ref[c]
            m_new = jnp.maximum(m_old, jnp.max(s, axis=1, keepdims=True))
            alpha = jnp.exp(m_old - m_new)
            p = jnp.exp(s - m_new)
            l_ref[c] = alpha * l_ref[c] + jnp.sum(p, axis=1, keepdims=True)
            acc_ref[c] = alpha * acc_ref[c] + _dot(p.astype(BF16), v)
            m_ref[c] = m_new

    def body(ki, carry):
        block(ki, False)
        return carry

    lax.fori_loop(0, qi, body, 0)
    block(qi, True)

    o = acc_ref[0] / l_ref[0] - lam_ref[0] * (acc_ref[1] / l_ref[1])
    o = _rms(o, sub_ref[...], SUBLN_EPS) * out_scale
    o_ref[0] = o.astype(BF16)


def _diff_attention(q, k, v, lam, subln_row, lam_init):
    bsz, s, _ = q.shape
    tq, tk = ATTN_TQ, ATTN_TK
    kern = functools.partial(_attn_kernel, tq=tq, tk=tk, out_scale=1.0 - lam_init)
    return pl.pallas_call(
        kern,
        grid=(bsz, N_HEADS, s // tq),
        in_specs=[pl.BlockSpec(memory_space=pltpu.SMEM),
                  pl.BlockSpec((1, tq, LANES), lambda b, h, i: (b, i, h)),
                  pl.BlockSpec((1, s, LANES), lambda b, h, i: (b, 0, h)),
                  pl.BlockSpec((1, s, LANES), lambda b, h, i: (b, 0, h)),
                  pl.BlockSpec((1, LANES), lambda b, h, i: (0, 0))],
        out_specs=pl.BlockSpec((1, tq, LANES), lambda b, h, i: (b, i, h)),
        out_shape=jax.ShapeDtypeStruct((bsz, s, ATTN_WIDTH), BF16),
        scratch_shapes=[pltpu.VMEM((2, tq, 1), F32),
                        pltpu.VMEM((2, tq, 1), F32),
                        pltpu.VMEM((2, tq, V_DIM), F32)],
        compiler_params=_params(("parallel", "parallel", "arbitrary")),
        name="diff_attention",
    )(lam, q, k, v, subln_row)


def _merge_kernel(*refs, with_router):
    if with_router:
        (x_ref, ya_ref, yb_ref, gs_ref, ga_ref, wps_ref, wpa_ref, wout_ref, fn_ref, wr_ref,
         x1_ref, h_ref, gate_ref) = refs
    else:
        (x_ref, ya_ref, yb_ref, gs_ref, ga_ref, wps_ref, wpa_ref, wout_ref, fn_ref,
         x1_ref, h_ref) = refs
    pa = _dot(ya_ref[...], wps_ref[...])
    pb = _dot(yb_ref[...], wpa_ref[...])
    m = gs_ref[...].astype(F32) * pa + ga_ref[...].astype(F32) * pb
    x1 = x_ref[...] + _dot(m.astype(BF16), wout_ref[...])
    x1_ref[...] = x1
    h = _rms(x1, fn_ref[...], EPS)
    h_ref[...] = h.astype(BF16)
    if with_router:
        w = wr_ref[...]
        w_hi = w.astype(BF16)
        w_lo = (w - w_hi.astype(F32)).astype(BF16)
        h_hi = h.astype(BF16)
        h_lo = (h - h_hi.astype(F32)).astype(BF16)
        logits = _dot(h_hi, w_hi) + (_dot(h_lo, w_hi) + _dot(h_hi, w_lo))
        lane = lax.broadcasted_iota(jnp.int32, logits.shape, 1)
        logits = jnp.where(lane < N_EXPERTS, logits, -jnp.inf)
        v1 = jnp.max(logits, axis=1, keepdims=True)
        i1 = jnp.min(jnp.where(logits == v1, lane, LANES), axis=1, keepdims=True)
        rest = jnp.where(lane == i1, -jnp.inf, logits)
        v2 = jnp.max(rest, axis=1, keepdims=True)
        i2 = jnp.min(jnp.where(rest == v2, lane, LANES), axis=1, keepdims=True)
        w1 = 1.0 / (1.0 + jnp.exp(v2 - v1))
        gate_ref[...] = jnp.where(lane == i1, w1, 0.0) + jnp.where(lane == i2, 1.0 - w1, 0.0)


def _merge(x, ya, yb, gs, ga, wps, wpa, wout, fn_row, router_pad=None):
    n = x.shape[0]
    tm = ROW_TILE
    with_router = router_pad is not None
    row = lambda w: pl.BlockSpec((tm, w), lambda i: (i, 0))
    full = lambda a: pl.BlockSpec(a.shape, lambda i: (0,) * a.ndim)
    ins = [x, ya, yb, gs, ga, wps, wpa, wout, fn_row]
    in_specs = [row(D_MODEL), row(SSM_WIDTH), row(ATTN_WIDTH), row(D_MODEL), row(D_MODEL),
                full(wps), full(wpa), full(wout), full(fn_row)]
    out_specs = [row(D_MODEL), row(D_MODEL)]
    out_shape = [jax.ShapeDtypeStruct((n, D_MODEL), F32), jax.ShapeDtypeStruct((n, D_MODEL), BF16)]
    if with_router:
        ins.append(router_pad)
        in_specs.append(full(router_pad))
        out_specs.append(row(LANES))
        out_shape.append(jax.ShapeDtypeStruct((n, LANES), F32))
    return pl.pallas_call(
        functools.partial(_merge_kernel, with_router=with_router),
        grid=(n // tm,),
        in_specs=in_specs, out_specs=out_specs, out_shape=out_shape,
        compiler_params=_params(("parallel",)),
        name="merge_router" if with_router else "merge",
    )(*ins)


def _swiglu_acc(h, wg_ref, wu_ref, wd_ref, idx):
    acc = None
    for c in range(D_FF // FF_CHUNK):
        sl = slice(c * FF_CHUNK, (c + 1) * FF_CHUNK)
        g = _dot(h, wg_ref[idx + (slice(None), sl)])
        up = _dot(h, wu_ref[idx + (slice(None), sl)])
        a = (g * jax.nn.sigmoid(g) * up).astype(BF16)
        part = _dot(a, wd_ref[idx + (sl, slice(None))])
        acc = part if acc is None else acc + part
    return acc


def _dense_ffn_kernel(h_ref, x1_ref, wg_ref, wu_ref, wd_ref, fin_ref, o_ref, *, final_norm):
    x2 = x1_ref[...] + _swiglu_acc(h_ref[...], wg_ref, wu_ref, wd_ref, ())
    o_ref[...] = _rms(x2, fin_ref[...], EPS) if final_norm else x2


def _dense_ffn(h, x1, wg, wu, wd, fin_row, final_norm):
    n = h.shape[0]
    tm = ROW_TILE
    row = lambda w: pl.BlockSpec((tm, w), lambda i: (i, 0))
    full = lambda a: pl.BlockSpec(a.shape, lambda i: (0,) * a.ndim)
    return pl.pallas_call(
        functools.partial(_dense_ffn_kernel, final_norm=final_norm),
        grid=(n // tm,),
        in_specs=[row(D_MODEL), row(D_MODEL), full(wg), full(wu), full(wd), full(fin_row)],
        out_specs=row(D_MODEL),
        out_shape=jax.ShapeDtypeStruct((n, D_MODEL), F32),
        compiler_params=_params(("parallel",)),
        name="dense_ffn",
    )(h, x1, wg, wu, wd, fin_row)


def _moe_ffn_kernel(h_ref, x1_ref, gate_ref, wg_ref, wu_ref, wd_ref, fin_ref, o_ref, acc_ref,
                    *, final_norm):
    e = pl.program_id(1)

    @pl.when(e == 0)
    def _():
        acc_ref[...] = jnp.zeros_like(acc_ref)

    lane = lax.broadcasted_iota(jnp.int32, gate_ref.shape, 1)
    gate = jnp.sum(jnp.where(lane == e, gate_ref[...], 0.0), axis=1, keepdims=True)
    acc_ref[...] += gate * _swiglu_acc(h_ref[...], wg_ref, wu_ref, wd_ref, (0,))

    @pl.when(e == pl.num_programs(1) - 1)
    def _():
        x2 = x1_ref[...] + acc_ref[...]
        o_ref[...] = _rms(x2, fin_ref[...], EPS) if final_norm else x2


def _moe_ffn(h, x1, gates, wg, wu, wd, fin_row, final_norm):
    n = h.shape[0]
    tm = ROW_TILE
    row = lambda w: pl.BlockSpec((tm, w), lambda i, e: (i, 0))
    return pl.pallas_call(
        functools.partial(_moe_ffn_kernel, final_norm=final_norm),
        grid=(n // tm, N_EXPERTS),
        in_specs=[row(D_MODEL), row(D_MODEL), row(LANES),
                  pl.BlockSpec((1, D_MODEL, D_FF), lambda i, e: (e, 0, 0)),
                  pl.BlockSpec((1, D_MODEL, D_FF), lambda i, e: (e, 0, 0)),
                  pl.BlockSpec((1, D_FF, D_MODEL), lambda i, e: (e, 0, 0)),
                  pl.BlockSpec(fin_row.shape, lambda i, e: (0, 0))],
        out_specs=row(D_MODEL),
        out_shape=jax.ShapeDtypeStruct((n, D_MODEL), F32),
        scratch_shapes=[pltpu.VMEM((tm, D_MODEL), F32)],
        compiler_params=_params(("parallel", "arbitrary")),
        name="moe_ffn",
    )(h, x1, gates, wg, wu, wd, fin_row)


def kernel(x, positions, mix_norm, w_in, ssm_lambda_re, ssm_lambda_im, ssm_log_dt, ssm_b_re, ssm_b_im,
           ssm_c_re, ssm_c_im, ssm_d, w_glu, b_glu, lambda_q1, lambda_k1, lambda_q2, lambda_k2, subln,
           w_proj_ssm, w_proj_attn, w_out, ffn_norm, dense_w_gate, dense_w_up, dense_w_down, router_w,
           moe_w_gate, moe_w_up, moe_w_down, final_norm):
    bsz, s, d = x.shape
    n = bsz * s
    depth = w_in.shape[0]

    inv_freq = ROPE_THETA ** (-jnp.arange(0, HEAD_DIM, 2, dtype=F32) / HEAD_DIM)
    ang = positions.astype(F32).reshape(n, 1) * inv_freq
    emb = jnp.concatenate([ang, ang, ang, ang], axis=-1)
    sign = jnp.where((jnp.arange(LANES) % HEAD_DIM) < HEAD_DIM // 2, -1.0, 1.0).astype(F32)
    cos_t = jnp.cos(emb)
    sin_t = jnp.sin(emb) * sign

    xf = x.reshape(n, d)
    fin_row = final_norm.reshape(1, d).astype(F32)
    for l in range(depth):
        u, q, k, v, gs, ga = _in_projection(
            xf, mix_norm[l].reshape(1, d).astype(F32), w_in[l].astype(BF16), cos_t, sin_t)

        bblk, cblk, a_vec = _s5_discretise(ssm_lambda_re[l], ssm_lambda_im[l], ssm_log_dt[l],
                                           ssm_b_re[l], ssm_b_im[l], ssm_c_re[l], ssm_c_im[l])
        u_tm = u.reshape(bsz, s, SSM_WIDTH).transpose(1, 0, 2).reshape(n, SSM_WIDTH)
        ya_tm = _s5_branch(u_tm, bblk, cblk, a_vec, ssm_d[l].reshape(1, SSM_WIDTH).astype(F32),
                           w_glu[l].astype(BF16), b_glu[l].reshape(1, SSM_WIDTH).astype(F32), bsz)
        ya = ya_tm.reshape(s, bsz, SSM_WIDTH).transpose(1, 0, 2).reshape(n, SSM_WIDTH)

        lam_init = 0.8 - 0.6 * math.exp(-0.3 * l)
        lam = (jnp.exp(jnp.sum(lambda_q1[l].astype(F32) * lambda_k1[l].astype(F32)))
               - jnp.exp(jnp.sum(lambda_q2[l].astype(F32) * lambda_k2[l].astype(F32))) + lam_init)
        yb = _diff_attention(q.reshape(bsz, s, QK_WIDTH), k.reshape(bsz, s, QK_WIDTH),
                             v.reshape(bsz, s, ATTN_WIDTH), lam.reshape(1).astype(F32),
                             subln[l].reshape(1, V_DIM).astype(F32), lam_init).reshape(n, ATTN_WIDTH)

        last = l == depth - 1
        fn_row = ffn_norm[l].reshape(1, d).astype(F32)
        wps = w_proj_ssm[l].astype(BF16)
        wpa = w_proj_attn[l].astype(BF16)
        wo = w_out[l].astype(BF16)
        i = l // 2
        if l % 2 == 0:
            x1, h = _merge(xf, ya, yb, gs, ga, wps, wpa, wo, fn_row)
            xf = _dense_ffn(h, x1, dense_w_gate[i].astype(BF16), dense_w_up[i].astype(BF16),
                            dense_w_down[i].astype(BF16), fin_row, last)
        else:
            router_pad = jnp.pad(router_w[i].astype(F32), ((0, 0), (0, LANES - N_EXPERTS)))
            x1, h, gates = _merge(xf, ya, yb, gs, ga, wps, wpa, wo, fn_row, router_pad)
            xf = _moe_ffn(h, x1, gates, moe_w_gate[i].astype(BF16), moe_w_up[i].astype(BF16),
                          moe_w_down[i].astype(BF16), fin_row, last)
    return xf.reshape(bsz, s, d)
```

```python
import functools
import math

import jax
import jax.numpy as jnp
from jax import lax
from jax.experimental import pallas as pl
from jax.experimental.pallas import tpu as pltpu

F32 = jnp.float32
BF16 = jnp.bfloat16

D_MODEL = 1024
SSM_GROUP = 16
SSM_GROUPS = 32
SSM_WIDTH = SSM_GROUP * SSM_GROUPS
SSM_STATE = 64
N_HEADS = 8
HEAD_DIM = 64
V_DIM = 2 * HEAD_DIM
QK_WIDTH = N_HEADS * 2 * HEAD_DIM
ATTN_WIDTH = N_HEADS * V_DIM
ROPE_THETA = 10000.0
IN_COLS = SSM_WIDTH + 2 * QK_WIDTH + ATTN_WIDTH + 2 * D_MODEL
D_FF = 2816
N_EXPERTS = 8
EPS = 1e-6
SUBLN_EPS = 1e-5

LANES = 128
HALF_GROUPS = SSM_GROUPS // 2
HALF_IN = HALF_GROUPS * SSM_GROUP
HALF_STATE = HALF_GROUPS * SSM_STATE
VMEM_LIMIT = 56 * 1024 * 1024

ROW_TILE = 512
S5_STEPS = 32
ATTN_TQ = 512
ATTN_TK = 512
ONES_ROWS = 16
LOG2_E = math.log2(math.e)
FF_CHUNK = 256
NEG_BIG = -1e30


def _dot(a, b):
    return jnp.dot(a, b, preferred_element_type=F32)


def _rms(x, g, eps):
    return x * lax.rsqrt(jnp.mean(x * x, axis=-1, keepdims=True) + eps) * g


def _params(sem):
    return pltpu.CompilerParams(dimension_semantics=sem, vmem_limit_bytes=VMEM_LIMIT)


def _inproj_kernel(x_ref, g_ref, w_ref, wvt_ref, cos_ref, sin_ref,
                   u_ref, q_ref, k_ref, vt_ref, gs_ref, ga_ref):
    h = _rms(x_ref[...], g_ref[...], EPS).astype(BF16)
    cos = cos_ref[...]
    sin = sin_ref[...]
    lane = lax.broadcasted_iota(jnp.int32, (1, LANES), 1)
    first_half = (lane % HEAD_DIM) < (HEAD_DIM // 2)

    def rope(t):
        partner = jnp.where(first_half,
                            pltpu.roll(t, LANES - HEAD_DIM // 2, 1),
                            pltpu.roll(t, HEAD_DIM // 2, 1))
        return t * cos + partner * sin

    u_ref[...] = _dot(h, w_ref[:, 0:SSM_WIDTH])
    chunk = 512
    q_scale = HEAD_DIM ** -0.5 * LOG2_E
    for c in range(QK_WIDTH // chunk):
        base = SSM_WIDTH + c * chunk
        zq = _dot(h, w_ref[:, base:base + chunk])
        zk = _dot(h, w_ref[:, base + QK_WIDTH:base + QK_WIDTH + chunk])
        for j in range(chunk // LANES):
            sl = slice(j * LANES, (j + 1) * LANES)
            dst = slice(c * chunk + j * LANES, c * chunk + (j + 1) * LANES)
            q_ref[:, dst] = (rope(zq[:, sl]) * q_scale).astype(BF16)
            k_ref[:, dst] = rope(zk[:, sl]).astype(BF16)
    for c in range(ATTN_WIDTH // chunk):
        sl = slice(c * chunk, (c + 1) * chunk)
        vt_ref[sl, :] = lax.dot_general(wvt_ref[sl, :], h, (((1,), (1,)), ((), ())),
                                        preferred_element_type=F32).astype(BF16)
    gbase = SSM_WIDTH + 2 * QK_WIDTH + ATTN_WIDTH
    for c in range(D_MODEL // chunk):
        sl = slice(c * chunk, (c + 1) * chunk)
        zs = _dot(h, w_ref[:, gbase + c * chunk:gbase + (c + 1) * chunk])
        za = _dot(h, w_ref[:, gbase + D_MODEL + c * chunk:gbase + D_MODEL + (c + 1) * chunk])
        gs_ref[:, sl] = jax.nn.sigmoid(zs).astype(BF16)
        ga_ref[:, sl] = jax.nn.sigmoid(za).astype(BF16)


def _in_projection(x, g, w_bf16, wvt_bf16, cos_t, sin_t):
    n = x.shape[0]
    tm = ROW_TILE
    row = lambda w: pl.BlockSpec((tm, w), lambda i: (i, 0))
    full = lambda a: pl.BlockSpec(a.shape, lambda i: (0,) * a.ndim)
    return pl.pallas_call(
        _inproj_kernel,
        grid=(n // tm,),
        in_specs=[row(D_MODEL), full(g), full(w_bf16), full(wvt_bf16), row(LANES), row(LANES)],
        out_specs=[row(SSM_WIDTH), row(QK_WIDTH), row(QK_WIDTH),
                   pl.BlockSpec((ATTN_WIDTH, tm), lambda i: (0, i)),
                   row(D_MODEL), row(D_MODEL)],
        out_shape=[jax.ShapeDtypeStruct((n, SSM_WIDTH), F32),
                   jax.ShapeDtypeStruct((n, QK_WIDTH), BF16),
                   jax.ShapeDtypeStruct((n, QK_WIDTH), BF16),
                   jax.ShapeDtypeStruct((ATTN_WIDTH, n), BF16),
                   jax.ShapeDtypeStruct((n, D_MODEL), BF16),
                   jax.ShapeDtypeStruct((n, D_MODEL), BF16)],
        compiler_params=_params(("parallel",)),
        name="in_projection",
    )(x, g, w_bf16, wvt_bf16, cos_t, sin_t)


def _gelu_tanh(x):
    return 0.5 * x * (1.0 + jnp.tanh(math.sqrt(2.0 / math.pi) * (x + 0.044715 * (x * x * x))))


def _s5_kernel(u_ref, bblk_ref, cblk_ref, a_ref, d_ref, wglu_ref, bglu_ref,
               ya_ref, state_ref, bu_ref, s_ref, *, steps, batch):
    @pl.when(pl.program_id(0) == 0)
    def _():
        state_ref[...] = jnp.zeros_like(state_ref)

    u = u_ref[...]
    ub = u.astype(BF16)
    half_cols = 2 * HALF_STATE
    for half in range(2):
        bu_ref[:, half * half_cols:(half + 1) * half_cols] = _dot(
            ub[:, half * HALF_IN:(half + 1) * HALF_IN], bblk_ref[half])

    width = 512
    for half in range(2):
        for j in range(HALF_STATE // width):
            cre = half * half_cols + j * width
            cim = cre + HALF_STATE
            a_re = jnp.broadcast_to(a_ref[2 * half:2 * half + 1, j * width:(j + 1) * width],
                                    (batch, width))
            a_im = jnp.broadcast_to(a_ref[2 * half + 1:2 * half + 2, j * width:(j + 1) * width],
                                    (batch, width))

            def step(t, carry, cre=cre, cim=cim, a_re=a_re, a_im=a_im):
                s_re, s_im = carry
                r0 = pl.multiple_of(t * batch, batch)
                n_re = a_re * s_re - a_im * s_im + bu_ref[pl.ds(r0, batch), cre:cre + width]
                n_im = a_re * s_im + a_im * s_re + bu_ref[pl.ds(r0, batch), cim:cim + width]
                s_ref[pl.ds(r0, batch), cre:cre + width] = n_re.astype(BF16)
                s_ref[pl.ds(r0, batch), cim:cim + width] = n_im.astype(BF16)
                return n_re, n_im

            s_re, s_im = lax.fori_loop(
                0, steps, step,
                (state_ref[:, cre:cre + width], state_ref[:, cim:cim + width]), unroll=4)
            state_ref[:, cre:cre + width] = s_re
            state_ref[:, cim:cim + width] = s_im

    y = jnp.concatenate(
        [_dot(s_ref[:, h * half_cols:(h + 1) * half_cols], cblk_ref[h]) for h in range(2)], axis=1)
    y = _gelu_tanh(y + d_ref[...] * u)
    z = _dot(y.astype(BF16), wglu_ref[...]) + bglu_ref[...]
    ya_ref[...] = (y * jax.nn.sigmoid(z)).astype(BF16)


def _s5_discretise(lam_re, lam_im, log_dt, b_re, b_im, c_re, c_im):
    lr = lam_re.astype(F32)
    li = lam_im.astype(F32)
    dt = jnp.exp(log_dt.astype(F32))[:, None]
    mag = jnp.exp(lr * dt)
    ar = mag * jnp.cos(li * dt)
    ai = mag * jnp.sin(li * dt)
    den = lr * lr + li * li
    nr = ar - 1.0
    ni = ai
    fr = (nr * lr + ni * li) / den
    fi = (ni * lr - nr * li) / den
    br = b_re.astype(F32)
    bi = b_im.astype(F32)
    bbr = fr[..., None] * br - fi[..., None] * bi
    bbi = fr[..., None] * bi + fi[..., None] * br
    eye = jnp.eye(HALF_GROUPS, dtype=F32)

    def in_block(w):
        w = w.reshape(2, HALF_GROUPS, SSM_STATE, SSM_GROUP)
        blk = jnp.einsum('hgpj,gk->hgjkp', w, eye)
        return blk.reshape(2, HALF_IN, HALF_STATE)

    def out_block(w):
        w = w.reshape(2, HALF_GROUPS, SSM_GROUP, SSM_STATE)
        blk = jnp.einsum('hgjp,gk->hgpkj', w, eye)
        return blk.reshape(2, HALF_STATE, HALF_IN)

    bblk = jnp.concatenate([in_block(bbr), in_block(bbi)], axis=2).astype(BF16)
    cblk = jnp.concatenate([out_block(c_re.astype(F32)), out_block(-c_im.astype(F32))],
                           axis=1).astype(BF16)
    a_vec = jnp.stack([ar.reshape(2, HALF_STATE)[0], ai.reshape(2, HALF_STATE)[0],
                       ar.reshape(2, HALF_STATE)[1], ai.reshape(2, HALF_STATE)[1]], axis=0)
    return bblk, cblk, a_vec


def _s5_branch(u_tm, bblk, cblk, a_vec, d_row, wglu_bf16, bglu_row, batch):
    n = u_tm.shape[0]
    rows = S5_STEPS * batch
    full = lambda a: pl.BlockSpec(a.shape, lambda i: (0,) * a.ndim)
    kern = functools.partial(_s5_kernel, steps=S5_STEPS, batch=batch)
    return pl.pallas_call(
        kern,
        grid=(n // rows,),
        in_specs=[pl.BlockSpec((rows, SSM_WIDTH), lambda i: (i, 0)),
                  full(bblk), full(cblk), full(a_vec), full(d_row), full(wglu_bf16), full(bglu_row)],
        out_specs=pl.BlockSpec((rows, SSM_WIDTH), lambda i: (i, 0)),
        out_shape=jax.ShapeDtypeStruct((n, SSM_WIDTH), BF16),
        scratch_shapes=[pltpu.VMEM((batch, 4 * HALF_STATE), F32),
                        pltpu.VMEM((rows, 4 * HALF_STATE), F32),
                        pltpu.VMEM((rows, 4 * HALF_STATE), BF16)],
        compiler_params=_params(("arbitrary",)),
        name="s5_branch",
    )(u_tm, bblk, cblk, a_vec, d_row, wglu_bf16, bglu_row)


def _attn_kernel(lam_ref, q_ref, k_ref, vt_ref, sub_ref, o_ref, m_ref, acc_ref,
                 *, tq, tk, out_scale):
    qi = pl.program_id(2)
    q = q_ref[0]
    lane = lax.broadcasted_iota(jnp.int32, (1, LANES), 1)
    zero = jnp.zeros_like(q)
    q_comp = (jnp.where(lane < HEAD_DIM, q, zero), jnp.where(lane >= HEAD_DIM, q, zero))
    ones = jnp.ones((ONES_ROWS, tk), BF16)

    m_ref[...] = jnp.full_like(m_ref, NEG_BIG)
    acc_ref[...] = jnp.zeros_like(acc_ref)

    def block(ki, masked):
        k0 = pl.multiple_of(ki * tk, tk)
        k = k_ref[0, pl.ds(k0, tk), :]
        vt = jnp.concatenate([vt_ref[:, pl.ds(k0, tk)], ones], axis=0)
        for c in range(2):
            st = lax.dot_general(k, q_comp[c], (((1,), (1,)), ((), ())),
                                 preferred_element_type=F32)
            if masked:
                key = lax.broadcasted_iota(jnp.int32, (tk, tq), 0)
                qry = lax.broadcasted_iota(jnp.int32, (tk, tq), 1)
                st = jnp.where(key <= qry, st, NEG_BIG)
            m_old = m_ref[c]
            m_new = jnp.maximum(m_old, jnp.max(st, axis=0, keepdims=True))
            alpha = jnp.exp2(m_old - m_new)
            p = jnp.exp2(st - m_new).astype(BF16)
            acc_ref[c] = alpha * acc_ref[c] + _dot(vt, p)
            m_ref[c] = m_new

    def body(ki, carry):
        block(ki, False)
        return carry

    lax.fori_loop(0, qi, body, 0)
    block(qi, True)

    a1 = acc_ref[0]
    a2 = acc_ref[1]
    o = (a1[:V_DIM] * (1.0 / a1[V_DIM:V_DIM + 1])
         - lam_ref[0] * (a2[:V_DIM] * (1.0 / a2[V_DIM:V_DIM + 1])))
    o = o * lax.rsqrt(jnp.mean(o * o, axis=0, keepdims=True) + SUBLN_EPS) * (sub_ref[...] * out_scale)
    o_ref[0] = o.T.astype(BF16)


def _diff_attention(q, k, vt, lam, subln_col, lam_init):
    bsz, s, _ = q.shape
    tq, tk = ATTN_TQ, ATTN_TK
    kern = functools.partial(_attn_kernel, tq=tq, tk=tk, out_scale=1.0 - lam_init)
    return pl.pallas_call(
        kern,
        grid=(bsz, N_HEADS, s // tq),
        in_specs=[pl.BlockSpec(memory_space=pltpu.SMEM),
                  pl.BlockSpec((1, tq, LANES), lambda b, h, i: (b, i, h)),
                  pl.BlockSpec((1, s, LANES), lambda b, h, i: (b, 0, h)),
                  pl.BlockSpec((V_DIM, s), lambda b, h, i: (h, b)),
                  pl.BlockSpec((V_DIM, 1), lambda b, h, i: (0, 0))],
        out_specs=pl.BlockSpec((1, tq, LANES), lambda b, h, i: (b, i, h)),
        out_shape=jax.ShapeDtypeStruct((bsz, s, ATTN_WIDTH), BF16),
        scratch_shapes=[pltpu.VMEM((2, 1, tq), F32),
                        pltpu.VMEM((2, V_DIM + ONES_ROWS, tq), F32)],
        compiler_params=_params(("parallel", "parallel", "arbitrary")),
        name="diff_attention",
    )(lam, q, k, vt, subln_col)


def _merge_kernel(*refs, with_router):
    if with_router:
        (x_ref, ya_ref, yb_ref, gs_ref, ga_ref, wps_ref, wpa_ref, wout_ref, fn_ref, wr_ref,
         x1_ref, h_ref, gate_ref) = refs
    else:
        (x_ref, ya_ref, yb_ref, gs_ref, ga_ref, wps_ref, wpa_ref, wout_ref, fn_ref,
         x1_ref, h_ref) = refs
    pa = _dot(ya_ref[...], wps_ref[...])
    pb = _dot(yb_ref[...], wpa_ref[...])
    m = gs_ref[...].astype(F32) * pa + ga_ref[...].astype(F32) * pb
    x1 = x_ref[...] + _dot(m.astype(BF16), wout_ref[...])
    x1_ref[...] = x1
    h = _rms(x1, fn_ref[...], EPS)
    h_ref[...] = h.astype(BF16)
    if with_router:
        w = wr_ref[...]
        w_hi = w.astype(BF16)
        w_lo = (w - w_hi.astype(F32)).astype(BF16)
        h_hi = h.astype(BF16)
        h_lo = (h - h_hi.astype(F32)).astype(BF16)
        logits = _dot(h_hi, w_hi) + (_dot(h_lo, w_hi) + _dot(h_hi, w_lo))
        lane = lax.broadcasted_iota(jnp.int32, logits.shape, 1)
        logits = jnp.where(lane < N_EXPERTS, logits, -jnp.inf)
        v1 = jnp.max(logits, axis=1, keepdims=True)
        i1 = jnp.min(jnp.where(logits == v1, lane, LANES), axis=1, keepdims=True)
        rest = jnp.where(lane == i1, -jnp.inf, logits)
        v2 = jnp.max(rest, axis=1, keepdims=True)
        i2 = jnp.min(jnp.where(rest == v2, lane, LANES), axis=1, keepdims=True)
        w1 = 1.0 / (1.0 + jnp.exp(v2 - v1))
        gate_ref[...] = jnp.where(lane == i1, w1, 0.0) + jnp.where(lane == i2, 1.0 - w1, 0.0)


def _merge(x, ya, yb, gs, ga, wps, wpa, wout, fn_row, router_pad=None):
    n = x.shape[0]
    tm = ROW_TILE
    with_router = router_pad is not None
    row = lambda w: pl.BlockSpec((tm, w), lambda i: (i, 0))
    full = lambda a: pl.BlockSpec(a.shape, lambda i: (0,) * a.ndim)
    ins = [x, ya, yb, gs, ga, wps, wpa, wout, fn_row]
    in_specs = [row(D_MODEL), row(SSM_WIDTH), row(ATTN_WIDTH), row(D_MODEL), row(D_MODEL),
                full(wps), full(wpa), full(wout), full(fn_row)]
    out_specs = [row(D_MODEL), row(D_MODEL)]
    out_shape = [jax.ShapeDtypeStruct((n, D_MODEL), F32), jax.ShapeDtypeStruct((n, D_MODEL), BF16)]
    if with_router:
        ins.append(router_pad)
        in_specs.append(full(router_pad))
        out_specs.append(row(LANES))
        out_shape.append(jax.ShapeDtypeStruct((n, LANES), F32))
    return pl.pallas_call(
        functools.partial(_merge_kernel, with_router=with_router),
        grid=(n // tm,),
        in_specs=in_specs, out_specs=out_specs, out_shape=out_shape,
        compiler_params=_params(("parallel",)),
        name="merge_router" if with_router else "merge",
    )(*ins)


def _swiglu_acc(h, wg_ref, wu_ref, wd_ref, idx):
    acc = None
    for c in range(D_FF // FF_CHUNK):
        sl = slice(c * FF_CHUNK, (c + 1) * FF_CHUNK)
        g = _dot(h, wg_ref[idx + (slice(None), sl)])
        up = _dot(h, wu_ref[idx + (slice(None), sl)])
        a = (g * jax.nn.sigmoid(g) * up).astype(BF16)
        part = _dot(a, wd_ref[idx + (sl, slice(None))])
        acc = part if acc is None else acc + part
    return acc


def _dense_ffn_kernel(h_ref, x1_ref, wg_ref, wu_ref, wd_ref, fin_ref, o_ref, *, final_norm):
    x2 = x1_ref[...] + _swiglu_acc(h_ref[...], wg_ref, wu_ref, wd_ref, ())
    o_ref[...] = _rms(x2, fin_ref[...], EPS) if final_norm else x2


def _dense_ffn(h, x1, wg, wu, wd, fin_row, final_norm):
    n = h.shape[0]
    tm = ROW_TILE
    row = lambda w: pl.BlockSpec((tm, w), lambda i: (i, 0))
    full = lambda a: pl.BlockSpec(a.shape, lambda i: (0,) * a.ndim)
    return pl.pallas_call(
        functools.partial(_dense_ffn_kernel, final_norm=final_norm),
        grid=(n // tm,),
        in_specs=[row(D_MODEL), row(D_MODEL), full(wg), full(wu), full(wd), full(fin_row)],
        out_specs=row(D_MODEL),
        out_shape=jax.ShapeDtypeStruct((n, D_MODEL), F32),
        compiler_params=_params(("parallel",)),
        name="dense_ffn",
    )(h, x1, wg, wu, wd, fin_row)


def _moe_ffn_kernel(h_ref, x1_ref, gate_ref, wg_ref, wu_ref, wd_ref, fin_ref, o_ref, acc_ref,
                    *, final_norm):
    e = pl.program_id(1)

    @pl.when(e == 0)
    def _():
        acc_ref[...] = jnp.zeros_like(acc_ref)

    lane = lax.broadcasted_iota(jnp.int32, gate_ref.shape, 1)
    gate = jnp.sum(jnp.where(lane == e, gate_ref[...], 0.0), axis=1, keepdims=True)
    acc_ref[...] += gate * _swiglu_acc(h_ref[...], wg_ref, wu_ref, wd_ref, (0,))

    @pl.when(e == pl.num_programs(1) - 1)
    def _():
        x2 = x1_ref[...] + acc_ref[...]
        o_ref[...] = _rms(x2, fin_ref[...], EPS) if final_norm else x2


def _moe_ffn(h, x1, gates, wg, wu, wd, fin_row, final_norm):
    n = h.shape[0]
    tm = ROW_TILE
    row = lambda w: pl.BlockSpec((tm, w), lambda i, e: (i, 0))
    return pl.pallas_call(
        functools.partial(_moe_ffn_kernel, final_norm=final_norm),
        grid=(n // tm, N_EXPERTS),
        in_specs=[row(D_MODEL), row(D_MODEL), row(LANES),
                  pl.BlockSpec((1, D_MODEL, D_FF), lambda i, e: (e, 0, 0)),
                  pl.BlockSpec((1, D_MODEL, D_FF), lambda i, e: (e, 0, 0)),
                  pl.BlockSpec((1, D_FF, D_MODEL), lambda i, e: (e, 0, 0)),
                  pl.BlockSpec(fin_row.shape, lambda i, e: (0, 0))],
        out_specs=row(D_MODEL),
        out_shape=jax.ShapeDtypeStruct((n, D_MODEL), F32),
        scratch_shapes=[pltpu.VMEM((tm, D_MODEL), F32)],
        compiler_params=_params(("parallel", "arbitrary")),
        name="moe_ffn",
    )(h, x1, gates, wg, wu, wd, fin_row)


def kernel(x, positions, mix_norm, w_in, ssm_lambda_re, ssm_lambda_im, ssm_log_dt, ssm_b_re, ssm_b_im,
           ssm_c_re, ssm_c_im, ssm_d, w_glu, b_glu, lambda_q1, lambda_k1, lambda_q2, lambda_k2, subln,
           w_proj_ssm, w_proj_attn, w_out, ffn_norm, dense_w_gate, dense_w_up, dense_w_down, router_w,
           moe_w_gate, moe_w_up, moe_w_down, final_norm):
    bsz, s, d = x.shape
    n = bsz * s
    depth = w_in.shape[0]

    inv_freq = ROPE_THETA ** (-jnp.arange(0, HEAD_DIM, 2, dtype=F32) / HEAD_DIM)
    ang = positions.astype(F32).reshape(n, 1) * inv_freq
    emb = jnp.concatenate([ang, ang, ang, ang], axis=-1)
    sign = jnp.where((jnp.arange(LANES) % HEAD_DIM) < HEAD_DIM // 2, -1.0, 1.0).astype(F32)
    cos_t = jnp.cos(emb)
    sin_t = jnp.sin(emb) * sign

    xf = x.reshape(n, d)
    fin_row = final_norm.reshape(1, d).astype(F32)
    for l in range(depth):
        vbase = SSM_WIDTH + 2 * QK_WIDTH
        u, q, k, vt, gs, ga = _in_projection(
            xf, mix_norm[l].reshape(1, d).astype(F32), w_in[l].astype(BF16),
            w_in[l][:, vbase:vbase + ATTN_WIDTH].T.astype(BF16), cos_t, sin_t)

        bblk, cblk, a_vec = _s5_discretise(ssm_lambda_re[l], ssm_lambda_im[l], ssm_log_dt[l],
                                           ssm_b_re[l], ssm_b_im[l], ssm_c_re[l], ssm_c_im[l])
        u_tm = u.reshape(bsz, s, SSM_WIDTH).transpose(1, 0, 2).reshape(n, SSM_WIDTH)
        ya_tm = _s5_branch(u_tm, bblk, cblk, a_vec, ssm_d[l].reshape(1, SSM_WIDTH).astype(F32),
                           w_glu[l].astype(BF16), b_glu[l].reshape(1, SSM_WIDTH).astype(F32), bsz)
        ya = ya_tm.reshape(s, bsz, SSM_WIDTH).transpose(1, 0, 2).reshape(n, SSM_WIDTH)

        lam_init = 0.8 - 0.6 * math.exp(-0.3 * l)
        lam = (jnp.exp(jnp.sum(lambda_q1[l].astype(F32) * lambda_k1[l].astype(F32)))
               - jnp.exp(jnp.sum(lambda_q2[l].astype(F32) * lambda_k2[l].astype(F32))) + lam_init)
        yb = _diff_attention(q.reshape(bsz, s, QK_WIDTH), k.reshape(bsz, s, QK_WIDTH),
                             vt, lam.reshape(1).astype(F32),
                             subln[l].reshape(V_DIM, 1).astype(F32), lam_init).reshape(n, ATTN_WIDTH)

        last = l == depth - 1
        fn_row = ffn_norm[l].reshape(1, d).astype(F32)
        wps = w_proj_ssm[l].astype(BF16)
        wpa = w_proj_attn[l].astype(BF16)
        wo = w_out[l].astype(BF16)
        i = l // 2
        if l % 2 == 0:
            x1, h = _merge(xf, ya, yb, gs, ga, wps, wpa, wo, fn_row)
            xf = _dense_ffn(h, x1, dense_w_gate[i].astype(BF16), dense_w_up[i].astype(BF16),
                            dense_w_down[i].astype(BF16), fin_row, last)
        else:
            router_pad = jnp.pad(router_w[i].astype(F32), ((0, 0), (0, LANES - N_EXPERTS)))
            x1, h, gates = _merge(xf, ya, yb, gs, ga, wps, wpa, wo, fn_row, router_pad)
            xf = _moe_ffn(h, x1, gates, moe_w_gate[i].astype(BF16), moe_w_up[i].astype(BF16),
                          moe_w_down[i].astype(BF16), fin_row, last)
    return xf.reshape(bsz, s, d)
```
